```python
import jax, jax.numpy as jnp
from jax import lax
import numpy as np

D_MODEL = 1024
BATCH = 8
SEQ = 8192
DEPTH = 2

MIX_WIDTH = D_MODEL
ATTN_WIDTH = MIX_WIDTH // 2
POOL_WIDTH = MIX_WIDTH - ATTN_WIDTH
HEAD_DIM = 64
N_HEADS = ATTN_WIDTH // HEAD_DIM
N_KV_HEADS = 2
GROUP = N_HEADS // N_KV_HEADS
KV_WIDTH = N_KV_HEADS * HEAD_DIM
WINDOW = 128
BLOCK = 128
ROT_DIM = HEAD_DIM // 4
ROPE_THETA = 500000.0
POOL_WINDOWS = (2, 4, 8, 16)
N_POOL_GROUPS = len(POOL_WINDOWS)
POOL_GROUP_WIDTH = POOL_WIDTH // N_POOL_GROUPS
IN_WIDTH = ATTN_WIDTH + 2 * KV_WIDTH + POOL_WIDTH
D_FF = ((int(np.ceil(8 * D_MODEL / 3)) + 255) // 256) * 256
N_MOD = 6
EPS = 1e-6
NEG_INF = -1e30

kernel_name = "hybrid_swa_sink_pool_swiglu_block"


def rms_norm(x, g):
    xf = x.astype(jnp.float32)
    y = xf * lax.rsqrt(jnp.mean(xf * xf, axis=-1, keepdims=True) + EPS)
    return (y * g.astype(jnp.float32)).astype(x.dtype)


def partial_rotary(t, positions):
    inv_freq = ROPE_THETA ** (-jnp.arange(0, ROT_DIM, 2, dtype=jnp.float32) / ROT_DIM)
    ang = positions.astype(jnp.float32)[:, :, None] * inv_freq
    cos = jnp.cos(ang)[:, :, None, :]
    sin = jnp.sin(ang)[:, :, None, :]
    tf = t.astype(jnp.float32)
    half = ROT_DIM // 2
    t1, t2, rest = tf[..., :half], tf[..., half:ROT_DIM], tf[..., ROT_DIM:]
    rot = jnp.concatenate([t1 * cos - t2 * sin, t2 * cos + t1 * sin, rest], axis=-1)
    return rot.astype(t.dtype)


def sliding_window_attention_with_sinks(q, k, v, sinks):
    B, S = q.shape[0], q.shape[1]
    nb = S // BLOCK
    qb = q.reshape(B, nb, BLOCK, N_KV_HEADS, GROUP, HEAD_DIM)
    kb = k.reshape(B, nb, BLOCK, N_KV_HEADS, HEAD_DIM)
    vb = v.reshape(B, nb, BLOCK, N_KV_HEADS, HEAD_DIM)
    pad = ((0, 0), (1, 0), (0, 0), (0, 0), (0, 0))
    k_cat = jnp.concatenate([jnp.pad(kb, pad)[:, :-1], kb], axis=2)
    v_cat = jnp.concatenate([jnp.pad(vb, pad)[:, :-1], vb], axis=2)
    scores = jnp.einsum("bnqkgd,bnskd->bnkgqs", qb, k_cat).astype(jnp.float32)
    scores = scores * (HEAD_DIM ** -0.5)
    qi = jnp.arange(BLOCK)[:, None]
    kj = jnp.arange(2 * BLOCK)[None, :]
    diff = qi + BLOCK - kj
    blk = jnp.arange(nb)[:, None, None]
    key_abs = blk * BLOCK + kj[None] - BLOCK
    valid = (diff[None] >= 0) & (diff[None] < WINDOW) & (key_abs >= 0)
    scores = jnp.where(valid[None, :, None, None], scores, NEG_INF)
    sink = jnp.broadcast_to(
        sinks.astype(jnp.float32).reshape(1, 1, N_KV_HEADS, GROUP, 1, 1),
        scores.shape[:-1] + (1,))
    probs = jax.nn.softmax(jnp.concatenate([scores, sink], axis=-1), axis=-1)[..., :-1]
    out = jnp.einsum("bnkgqs,bnskd->bnqkgd", probs.astype(v.dtype), v_cat)
    return out.reshape(B, S, N_HEADS * HEAD_DIM)


def causal_pool_mixer(u, pool_w, pool_scale):
    S = u.shape[1]
    t = jnp.arange(S, dtype=jnp.float32)[None, :, None]
    outs = []
    for gi, w in enumerate(POOL_WINDOWS):
        ug = u[..., gi * POOL_GROUP_WIDTH:(gi + 1) * POOL_GROUP_WIDTH].astype(jnp.float32)
        cs = jnp.pad(jnp.cumsum(ug, axis=1), ((0, 0), (1, 0), (0, 0)))
        upper = cs[:, 1:]
        lower = jnp.pad(cs, ((0, 0), (w - 1, 0), (0, 0)))[:, :S]
        count = jnp.minimum(t + 1.0, float(w))
        pooled = (upper - lower) / count - ug
        outs.append(jnp.einsum("bsc,cd->bsd", pooled.astype(u.dtype), pool_w[gi]))
    return jnp.concatenate(outs, axis=-1) * pool_scale


def setup_inputs(seed: int = 0) -> dict:
    key = jax.random.key(seed)
    ks = jax.random.split(key, 20)
    f32 = jnp.float32
    def nrm(k, shape, scale):
        return jax.random.normal(k, shape, f32) * scale
    x = jax.random.normal(ks[0], (BATCH, SEQ, D_MODEL), f32)
    c = jax.random.normal(ks[1], (BATCH, D_MODEL), f32)
    offsets = jax.random.randint(ks[2], (BATCH, 1), 0, 4096, dtype=jnp.int32)
    positions = (offsets + jnp.arange(SEQ, dtype=jnp.int32)[None, :]).astype(jnp.int32)
    return {
        "x": x,
        "c": c,
        "positions": positions,
        "ada_w": nrm(ks[3], (DEPTH, D_MODEL, N_MOD * D_MODEL), D_MODEL ** -0.5),
        "ada_b": nrm(ks[4], (DEPTH, N_MOD * D_MODEL), 0.02),
        "w_in": nrm(ks[5], (DEPTH, D_MODEL, IN_WIDTH), D_MODEL ** -0.5),
        "b_in": nrm(ks[6], (DEPTH, IN_WIDTH), 0.02),
        "sinks": nrm(ks[7], (DEPTH, N_HEADS), 1.0),
        "pool_w": nrm(ks[8], (DEPTH, N_POOL_GROUPS, POOL_GROUP_WIDTH, POOL_GROUP_WIDTH), POOL_GROUP_WIDTH ** -0.5),
        "pool_scale": 1.0 + nrm(ks[9], (DEPTH, POOL_WIDTH), 0.1),
        "w_out": nrm(ks[10], (DEPTH, MIX_WIDTH, D_MODEL), MIX_WIDTH ** -0.5),
        "w_gate": nrm(ks[11], (DEPTH, D_MODEL, D_FF), D_MODEL ** -0.5),
        "w_up": nrm(ks[12], (DEPTH, D_MODEL, D_FF), D_MODEL ** -0.5),
        "w_down": nrm(ks[13], (DEPTH, D_FF, D_MODEL), D_FF ** -0.5),
        "g_pre_mix": 1.0 + nrm(ks[14], (DEPTH, D_MODEL), 0.02),
        "g_post_mix": 1.0 + nrm(ks[15], (DEPTH, D_MODEL), 0.02),
        "g_pre_ffn": 1.0 + nrm(ks[16], (DEPTH, D_MODEL), 0.02),
        "g_post_ffn": 1.0 + nrm(ks[17], (DEPTH, D_MODEL), 0.02),
    }


def reference(x, c, positions, ada_w, ada_b, w_in, b_in, sinks, pool_w, pool_scale,
              w_out, w_gate, w_up, w_down, g_pre_mix, g_post_mix, g_pre_ffn, g_post_ffn):
    B, S = x.shape[0], x.shape[1]
    c_act = jax.nn.silu(c)
    for l in range(DEPTH):
        mod = c_act @ ada_w[l] + ada_b[l]
        shift_m, scale_m, gate_m, shift_f, scale_f, gate_f = [
            m[:, None, :] for m in jnp.split(mod, N_MOD, axis=-1)]

        h = rms_norm(x, g_pre_mix[l]) * (1.0 + scale_m) + shift_m
        proj = h @ w_in[l] + b_in[l]
        q, k, v, u = jnp.split(
            proj, [ATTN_WIDTH, ATTN_WIDTH + KV_WIDTH, ATTN_WIDTH + 2 * KV_WIDTH], axis=-1)
        q = partial_rotary(q.reshape(B, S, N_HEADS, HEAD_DIM), positions)
        k = partial_rotary(k.reshape(B, S, N_KV_HEADS, HEAD_DIM), positions)
        v = v.reshape(B, S, N_KV_HEADS, HEAD_DIM)
        attn_out = sliding_window_attention_with_sinks(q, k, v, sinks[l])
        pool_out = causal_pool_mixer(u, pool_w[l], pool_scale[l])
        mix = jnp.concatenate([attn_out, pool_out], axis=-1) @ w_out[l]
        x = x + gate_m * rms_norm(mix, g_post_mix[l])

        h = rms_norm(x, g_pre_ffn[l]) * (1.0 + scale_f) + shift_f
        f = (jax.nn.silu(h @ w_gate[l]) * (h @ w_up[l])) @ w_down[l]
        x = x + gate_f * rms_norm(f, g_post_ffn[l])
    return x
```

```python
import functools

import numpy as np
import jax
import jax.numpy as jnp
from jax import lax
from jax.experimental import pallas as pl
from jax.experimental.pallas import tpu as pltpu

D_MODEL = 1024
DEPTH = 2
HEAD_DIM = 64
N_HEADS = 8
N_KV_HEADS = 2
GROUP = N_HEADS // N_KV_HEADS
ATTN_WIDTH = N_HEADS * HEAD_DIM
KV_WIDTH = N_KV_HEADS * HEAD_DIM
POOL_WIDTH = 512
BLOCK = 128
ROT_DIM = HEAD_DIM // 4
ROT_HALF = ROT_DIM // 2
ROPE_THETA = 500000.0
POOL_WINDOWS = (2, 4, 8, 16)
POOL_GROUP_WIDTH = POOL_WIDTH // len(POOL_WINDOWS)
POOL_HALO = 16
IN_WIDTH = ATTN_WIDTH + 2 * KV_WIDTH + POOL_WIDTH
D_FF = 2816
N_MOD = 6
EPS = 1e-6
NEG_INF = -1e30

LANES = 128
VMEM_LIMIT_BYTES = 56 * 1024 * 1024

MIXER_TILE = 512
FFN_TILE = 512
FFN_CHUNK = 256
MOD_TILE = 1536

F32 = jnp.float32
BF16 = jnp.bfloat16


def _rope_expand_matrix():
    e = np.zeros((4 * ROT_HALF, 2 * LANES), np.float32)
    for lane in range(LANES):
        d = lane % HEAD_DIM
        if d < ROT_HALF:
            f, sign = d, -1.0
        elif d < ROT_DIM:
            f, sign = d - ROT_HALF, 1.0
        else:
            continue
        e[f, lane] = 1.0
        e[ROT_HALF + f, lane] = 1.0
        e[2 * ROT_HALF + f, LANES + lane] = sign
        e[3 * ROT_HALF + f, LANES + lane] = sign
    return e


def _band_bias():
    qi = np.arange(BLOCK)[:, None]
    kj = np.arange(2 * BLOCK)[None, :]
    diff = qi + BLOCK - kj
    valid = (diff >= 0) & (diff < BLOCK)
    first = valid & (kj >= BLOCK)
    out = np.stack([np.where(valid, 0.0, NEG_INF), np.where(first, 0.0, NEG_INF)])
    return out.astype(np.float32)


def _rms(x):
    return x * lax.rsqrt(jnp.mean(x * x, axis=-1, keepdims=True) + EPS)


def _mod_kernel(c_ref, w_ref, b_ref, o_ref):
    c = c_ref[...]
    c_act = (c * (1.0 / (1.0 + jnp.exp(-c)))).astype(BF16)
    w = w_ref[...].astype(BF16)
    o_ref[...] = jnp.dot(c_act, w, preferred_element_type=F32) + b_ref[...]


def _modulation(c, ada_w, ada_b):
    batch = c.shape[0]
    n_out = N_MOD * D_MODEL
    return pl.pallas_call(
        _mod_kernel,
        grid=(DEPTH, n_out // MOD_TILE),
        in_specs=[
            pl.BlockSpec((batch, D_MODEL), lambda l, j: (0, 0)),
            pl.BlockSpec((None, D_MODEL, MOD_TILE), lambda l, j: (l, 0, j)),
            pl.BlockSpec((None, 1, MOD_TILE), lambda l, j: (l, 0, j)),
        ],
        out_specs=pl.BlockSpec((None, batch, MOD_TILE), lambda l, j: (l, 0, j)),
        out_shape=jax.ShapeDtypeStruct((DEPTH, batch, n_out), F32),
        compiler_params=pltpu.CompilerParams(
            dimension_semantics=("arbitrary", "arbitrary"),
            vmem_limit_bytes=VMEM_LIMIT_BYTES),
        name="modulation",
    )(c, ada_w, ada_b.reshape(DEPTH, 1, n_out))


def _mixer_kernel(sinks_ref, x_ref, mod_ref, pos_ref, win_ref, bin_ref, poolw_ref,
                  pscale_ref, wout_ref, gpre_ref, gpost_ref, rope_e_ref, bias_ref,
                  o_ref, kd_ref, vd_ref, ubuf_ref, cat_ref, *, tile, layer):
    s_idx = pl.program_id(1)
    n_blk = tile // BLOCK

    @pl.when(s_idx == 0)
    def _():
        kd_ref[:, 0:BLOCK, :] = jnp.zeros((N_KV_HEADS, BLOCK, LANES), BF16)
        vd_ref[:, 0:BLOCK, :] = jnp.zeros((N_KV_HEADS, BLOCK, LANES), BF16)
        ubuf_ref[0:POOL_HALO, :] = jnp.zeros((POOL_HALO, POOL_WIDTH), F32)

    x = x_ref[...]
    mod = mod_ref[...]
    shift, scale, gate = mod[0:1], mod[1:2], mod[2:3]

    h = _rms(x) * (gpre_ref[...] * (1.0 + scale)) + shift
    proj = jnp.dot(h.astype(BF16), win_ref[...], preferred_element_type=F32) + bin_ref[...]

    pos = pos_ref[...].astype(F32)
    f_idx = lax.broadcasted_iota(jnp.int32, (ROT_HALF, 1), 0).astype(F32)
    inv_freq = jnp.power(jnp.float32(ROPE_THETA), -(2.0 * f_idx) / ROT_DIM)
    ang = inv_freq * pos
    cos_a, sin_a = jnp.cos(ang), jnp.sin(ang)
    cos_hi = cos_a.astype(BF16).astype(F32)
    sin_hi = sin_a.astype(BF16).astype(F32)
    trig_t = jnp.concatenate([cos_hi, cos_a - cos_hi, sin_hi, sin_a - sin_hi], axis=0)
    table = lax.dot_general(trig_t.astype(BF16), rope_e_ref[...],
                            (((0,), (0,)), ((), ())), preferred_element_type=F32)
    lane = lax.broadcasted_iota(jnp.int32, (1, LANES), 1)
    head_d = lane & (HEAD_DIM - 1)
    cos_t = table[:, 0:LANES] + jnp.where(head_d >= ROT_DIM, 1.0, 0.0)
    sin_t = table[:, LANES:2 * LANES]
    first_half = head_d < ROT_HALF
    lo_lanes = lane < HEAD_DIM

    def rotary(t):
        partner = jnp.where(first_half, pltpu.roll(t, LANES - ROT_HALF, 1),
                            pltpu.roll(t, ROT_HALF, 1))
        return t * cos_t + partner * sin_t

    k_rot = rotary(proj[:, ATTN_WIDTH:ATTN_WIDTH + KV_WIDTH])
    v = proj[:, ATTN_WIDTH + KV_WIDTH:ATTN_WIDTH + 2 * KV_WIDTH]
    k_sw = pltpu.roll(k_rot, HEAD_DIM, 1)
    v_sw = pltpu.roll(v, HEAD_DIM, 1)
    kd_ref[0, BLOCK:, :] = jnp.where(lo_lanes, k_rot, k_sw).astype(BF16)
    kd_ref[1, BLOCK:, :] = jnp.where(lo_lanes, k_sw, k_rot).astype(BF16)
    vd_ref[0, BLOCK:, :] = jnp.where(lo_lanes, v, v_sw).astype(BF16)
    vd_ref[1, BLOCK:, :] = jnp.where(lo_lanes, v_sw, v).astype(BF16)

    q_heads = []
    for c in range(ATTN_WIDTH // LANES):
        qc = rotary(proj[:, c * LANES:(c + 1) * LANES]) * (HEAD_DIM ** -0.5)
        q_heads.append(jnp.where(lo_lanes, qc, 0.0).astype(BF16))
        q_heads.append(jnp.where(lo_lanes, 0.0, qc).astype(BF16))

    bias_mid = bias_ref[0]
    bias_first = jnp.where(s_idx == 0, bias_ref[1], bias_mid)
    for g in range(N_KV_HEADS):
        sink_col = jnp.concatenate(
            [jnp.full((BLOCK, 1), sinks_ref[layer, g * GROUP + i], F32) for i in range(GROUP)],
            axis=0)
        for j in range(n_blk):
            rows = slice(j * BLOCK, (j + 1) * BLOCK)
            q4 = jnp.concatenate([q_heads[g * GROUP + i][rows] for i in range(GROUP)], axis=0)
            k_cat = kd_ref[g, j * BLOCK:(j + 2) * BLOCK, :]
            v_cat = vd_ref[g, j * BLOCK:(j + 2) * BLOCK, :]
            s = lax.dot_general(q4, k_cat, (((1,), (1,)), ((), ())),
                                preferred_element_type=F32)
            bias = bias_first if j == 0 else bias_mid
            s = s + jnp.concatenate([bias] * GROUP, axis=0)
            m = jnp.maximum(jnp.max(s, axis=-1, keepdims=True), sink_col)
            p = jnp.exp(s - m)
            denom = jnp.sum(p, axis=-1, keepdims=True) + jnp.exp(sink_col - m)
            pv = jnp.dot(p.astype(BF16), v_cat, preferred_element_type=F32)
            o = pv * (1.0 / denom)
            for pair in range(GROUP // 2):
                even = o[(2 * pair) * BLOCK:(2 * pair + 1) * BLOCK]
                odd = o[(2 * pair + 1) * BLOCK:(2 * pair + 2) * BLOCK]
                col = (g * GROUP // 2 + pair) * LANES
                cat_ref[rows, col:col + LANES] = jnp.where(lo_lanes, even, odd).astype(BF16)

    u = proj[:, ATTN_WIDTH + 2 * KV_WIDTH:]
    ubuf_ref[POOL_HALO:, :] = u
    t_abs = lax.broadcasted_iota(jnp.int32, (tile, LANES), 0) + s_idx * tile
    for gi, w in enumerate(POOL_WINDOWS):
        cols = slice(gi * POOL_GROUP_WIDTH, (gi + 1) * POOL_GROUP_WIDTH)
        ug = u[:, cols]
        win_sum = ug
        for back in range(1, w):
            win_sum = win_sum + ubuf_ref[POOL_HALO - back:POOL_HALO - back + tile, cols]
        count = jnp.minimum(t_abs + 1, w).astype(F32)
        pooled = win_sum / count - ug
        po = jnp.dot(pooled.astype(BF16), poolw_ref[gi], preferred_element_type=F32)
        cat_ref[:, ATTN_WIDTH + gi * POOL_GROUP_WIDTH:ATTN_WIDTH + (gi + 1) * POOL_GROUP_WIDTH] = (
            po * pscale_ref[:, cols]).astype(BF16)

    kd_ref[:, 0:BLOCK, :] = kd_ref[:, tile:tile + BLOCK, :]
    vd_ref[:, 0:BLOCK, :] = vd_ref[:, tile:tile + BLOCK, :]
    ubuf_ref[0:POOL_HALO, :] = ubuf_ref[tile:tile + POOL_HALO, :]

    mix = jnp.dot(cat_ref[...], wout_ref[...], preferred_element_type=F32)
    o_ref[...] = x + gate * (_rms(mix) * gpost_ref[...])


def _mixer(layer, x, mod, pos3, w_in, b_in, sinks, pool_w, pool_scale, w_out,
           g_pre, g_post, rope_e, bias):
    batch, seq, _ = x.shape
    tile = MIXER_TILE
    const = lambda *shape: pl.BlockSpec(shape, lambda b, s: (0,) * len(shape))
    layer_spec = lambda *shape: pl.BlockSpec((None,) + shape, lambda b, s: (layer,) + (0,) * len(shape))
    return pl.pallas_call(
        functools.partial(_mixer_kernel, tile=tile, layer=layer),
        grid=(batch, seq // tile),
        in_specs=[
            pl.BlockSpec(memory_space=pltpu.SMEM),
            pl.BlockSpec((None, tile, D_MODEL), lambda b, s: (b, s, 0)),
            pl.BlockSpec((None, None, N_MOD, D_MODEL), lambda b, s: (layer, b, 0, 0)),
            pl.BlockSpec((None, 1, tile), lambda b, s: (b, 0, s)),
            layer_spec(D_MODEL, IN_WIDTH),
            layer_spec(1, IN_WIDTH),
            layer_spec(len(POOL_WINDOWS), POOL_GROUP_WIDTH, POOL_GROUP_WIDTH),
            layer_spec(1, POOL_WIDTH),
            layer_spec(D_MODEL, D_MODEL),
            layer_spec(1, D_MODEL),
            layer_spec(1, D_MODEL),
            const(4 * ROT_HALF, 2 * LANES),
            const(2, BLOCK, 2 * BLOCK),
        ],
        out_specs=pl.BlockSpec((None, tile, D_MODEL), lambda b, s: (b, s, 0)),
        out_shape=jax.ShapeDtypeStruct(x.shape, F32),
        scratch_shapes=[
            pltpu.VMEM((N_KV_HEADS, BLOCK + tile, LANES), BF16),
            pltpu.VMEM((N_KV_HEADS, BLOCK + tile, LANES), BF16),
            pltpu.VMEM((POOL_HALO + tile, POOL_WIDTH), F32),
            pltpu.VMEM((tile, D_MODEL), BF16),
        ],
        compiler_params=pltpu.CompilerParams(
            dimension_semantics=("arbitrary", "arbitrary"),
            vmem_limit_bytes=VMEM_LIMIT_BYTES),
        name=f"mixer_l{layer}",
    )(sinks, x, mod, pos3, w_in, b_in, pool_w, pool_scale, w_out, g_pre, g_post, rope_e, bias)


def _ffn_kernel(x_ref, mod_ref, wg_ref, wu_ref, wd_ref, gpre_ref, gpost_ref, o_ref, acc_ref):
    x = x_ref[...]
    mod = mod_ref[...]
    shift, scale, gate = mod[3:4], mod[4:5], mod[5:6]
    h = (_rms(x) * (gpre_ref[...] * (1.0 + scale)) + shift).astype(BF16)
    for idx, c0 in enumerate(range(0, D_FF, FFN_CHUNK)):
        cols = slice(c0, c0 + FFN_CHUNK)
        gate_h = jnp.dot(h, wg_ref[:, cols], preferred_element_type=F32)
        up_h = jnp.dot(h, wu_ref[:, cols], preferred_element_type=F32)
        act = (gate_h * (1.0 / (1.0 + jnp.exp(-gate_h))) * up_h).astype(BF16)
        part = jnp.dot(act, wd_ref[cols, :], preferred_element_type=F32)
        if idx == 0:
            acc_ref[...] = part
        else:
            acc_ref[...] += part
    o_ref[...] = x + gate * (_rms(acc_ref[...]) * gpost_ref[...])


def _ffn(layer, x, mod, w_gate, w_up, w_down, g_pre, g_post):
    batch, seq, _ = x.shape
    tile = FFN_TILE
    resident = lambda *shape: pl.BlockSpec(
        (None,) + shape, lambda b, s: (layer,) + (0,) * len(shape), pipeline_mode=pl.Buffered(1))
    return pl.pallas_call(
        _ffn_kernel,
        grid=(batch, seq // tile),
        in_specs=[
            pl.BlockSpec((None, tile, D_MODEL), lambda b, s: (b, s, 0)),
            pl.BlockSpec((None, None, N_MOD, D_MODEL), lambda b, s: (layer, b, 0, 0)),
            resident(D_MODEL, D_FF),
            resident(D_MODEL, D_FF),
            resident(D_FF, D_MODEL),
            resident(1, D_MODEL),
            resident(1, D_MODEL),
        ],
        out_specs=pl.BlockSpec((None, tile, D_MODEL), lambda b, s: (b, s, 0)),
        out_shape=jax.ShapeDtypeStruct(x.shape, F32),
        scratch_shapes=[pltpu.VMEM((tile, D_MODEL), F32)],
        compiler_params=pltpu.CompilerParams(
            dimension_semantics=("arbitrary", "arbitrary"),
            vmem_limit_bytes=VMEM_LIMIT_BYTES),
        name=f"ffn_l{layer}",
    )(x, mod, w_gate, w_up, w_down, g_pre, g_post)


def kernel(x, c, positions, ada_w, ada_b, w_in, b_in, sinks, pool_w, pool_scale,
           w_out, w_gate, w_up, w_down, g_pre_mix, g_post_mix, g_pre_ffn, g_post_ffn):
    batch, seq, d_model = x.shape
    assert d_model == D_MODEL and seq % MIXER_TILE == 0 and seq % FFN_TILE == 0
    assert w_in.shape == (DEPTH, D_MODEL, IN_WIDTH) and w_gate.shape == (DEPTH, D_MODEL, D_FF)

    mod = _modulation(c, ada_w, ada_b).reshape(DEPTH, batch, N_MOD, D_MODEL)
    pos3 = positions.reshape(batch, 1, seq)
    rope_e = jnp.asarray(_rope_expand_matrix(), BF16)
    bias = jnp.asarray(_band_bias())
    row = lambda a: a.reshape(DEPTH, 1, a.shape[-1])

    w_in_b, w_out_b, pool_w_b = w_in.astype(BF16), w_out.astype(BF16), pool_w.astype(BF16)
    w_gate_b, w_up_b, w_down_b = w_gate.astype(BF16), w_up.astype(BF16), w_down.astype(BF16)
    for layer in range(DEPTH):
        x = _mixer(layer, x, mod, pos3, w_in_b, row(b_in), sinks, pool_w_b, row(pool_scale),
                   w_out_b, row(g_pre_mix), row(g_post_mix), rope_e, bias)
        x = _ffn(layer, x, mod, w_gate_b, w_up_b, w_down_b, row(g_pre_ffn), row(g_post_ffn))
    return x
```

```python
import functools

import numpy as np
import jax
import jax.numpy as jnp
from jax import lax
from jax.experimental import pallas as pl
from jax.experimental.pallas import tpu as pltpu

D_MODEL = 1024
DEPTH = 2
HEAD_DIM = 64
N_HEADS = 8
N_KV_HEADS = 2
GROUP = N_HEADS // N_KV_HEADS
ATTN_WIDTH = N_HEADS * HEAD_DIM
KV_WIDTH = N_KV_HEADS * HEAD_DIM
POOL_WIDTH = 512
BLOCK = 128
ROT_DIM = HEAD_DIM // 4
ROT_HALF = ROT_DIM // 2
ROPE_THETA = 500000.0
POOL_WINDOWS = (2, 4, 8, 16)
POOL_GROUP_WIDTH = POOL_WIDTH // len(POOL_WINDOWS)
IN_WIDTH = ATTN_WIDTH + 2 * KV_WIDTH + POOL_WIDTH
D_FF = 2816
N_MOD = 6
EPS = 1e-6
NEG_INF = -1e30

LANES = 128
SUBLANES = 8
VMEM_LIMIT_BYTES = 56 * 1024 * 1024

MIXER_TILE = 512
FFN_TILE = 512
FFN_CHUNK = 256
MOD_TILE = 1536

F32 = jnp.float32
BF16 = jnp.bfloat16


def _rope_expand_matrix():
    e = np.zeros((4 * ROT_HALF, 2 * LANES), np.float32)
    for lane in range(LANES):
        d = lane % HEAD_DIM
        if d < ROT_HALF:
            f, sign = d, -1.0
        elif d < ROT_DIM:
            f, sign = d - ROT_HALF, 1.0
        else:
            continue
        e[f, lane] = 1.0
        e[ROT_HALF + f, lane] = 1.0
        e[2 * ROT_HALF + f, LANES + lane] = sign
        e[3 * ROT_HALF + f, LANES + lane] = sign
    return e


def _band_bias_t():
    kj = np.arange(2 * BLOCK)[:, None]
    qi = np.arange(BLOCK)[None, :]
    diff = qi + BLOCK - kj
    valid = (diff >= 0) & (diff < BLOCK)
    first = valid & (kj >= BLOCK)
    out = np.stack([np.where(valid, 0.0, NEG_INF), np.where(first, 0.0, NEG_INF)])
    return out.astype(np.float32)


def _rms(x):
    return x * lax.rsqrt(jnp.mean(x * x, axis=-1, keepdims=True) + EPS)


def _mod_kernel(c_ref, w_ref, b_ref, o_ref):
    c = c_ref[...]
    c_act = (c * (1.0 / (1.0 + jnp.exp(-c)))).astype(BF16)
    w = w_ref[...].astype(BF16)
    o_ref[...] = jnp.dot(c_act, w, preferred_element_type=F32) + b_ref[...]


def _modulation(c, ada_w, ada_b):
    batch = c.shape[0]
    n_out = N_MOD * D_MODEL
    return pl.pallas_call(
        _mod_kernel,
        grid=(DEPTH, n_out // MOD_TILE),
        in_specs=[
            pl.BlockSpec((batch, D_MODEL), lambda l, j: (0, 0)),
            pl.BlockSpec((None, D_MODEL, MOD_TILE), lambda l, j: (l, 0, j)),
            pl.BlockSpec((None, 1, MOD_TILE), lambda l, j: (l, 0, j)),
        ],
        out_specs=pl.BlockSpec((None, batch, MOD_TILE), lambda l, j: (l, 0, j)),
        out_shape=jax.ShapeDtypeStruct((DEPTH, batch, n_out), F32),
        compiler_params=pltpu.CompilerParams(
            dimension_semantics=("arbitrary", "arbitrary"),
            vmem_limit_bytes=VMEM_LIMIT_BYTES),
        name="modulation",
    )(c, ada_w, ada_b.reshape(DEPTH, 1, n_out))


def _mixer_kernel(sinks_ref, x_ref, mod_ref, pos_ref, win_ref, bin_ref, poolw_ref,
                  pscale_ref, wout_ref, gpre_ref, gpost_ref, rope_e_ref, bias_ref,
                  o_ref, kd_ref, vt_ref, pool_ref, cat_ref, *, tile, layer):
    s_idx = pl.program_id(1)
    n_blk = tile // BLOCK
    n_stage = len(POOL_WINDOWS)

    @pl.when(s_idx == 0)
    def _():
        kd_ref[:, 0:BLOCK, :] = jnp.zeros((N_KV_HEADS, BLOCK, LANES), BF16)
        vt_ref[:, 0:BLOCK] = jnp.zeros((KV_WIDTH, BLOCK), BF16)
        pool_ref[:, 0:SUBLANES, :] = jnp.zeros((n_stage, SUBLANES, POOL_WIDTH), F32)

    x = x_ref[...]
    mod = mod_ref[...]
    shift, scale, gate = mod[0:1], mod[1:2], mod[2:3]

    h = _rms(x) * (gpre_ref[...] * (1.0 + scale)) + shift
    proj = jnp.dot(h.astype(BF16), win_ref[...], preferred_element_type=F32) + bin_ref[...]

    pos = pos_ref[...].astype(F32)
    f_idx = lax.broadcasted_iota(jnp.int32, (ROT_HALF, 1), 0).astype(F32)
    inv_freq = jnp.power(jnp.float32(ROPE_THETA), -(2.0 * f_idx) / ROT_DIM)
    ang = inv_freq * pos
    cos_a, sin_a = jnp.cos(ang), jnp.sin(ang)
    cos_hi = cos_a.astype(BF16).astype(F32)
    sin_hi = sin_a.astype(BF16).astype(F32)
    trig_t = jnp.concatenate([cos_hi, cos_a - cos_hi, sin_hi, sin_a - sin_hi], axis=0)
    table = lax.dot_general(trig_t.astype(BF16), rope_e_ref[...],
                            (((0,), (0,)), ((), ())), preferred_element_type=F32)
    lane = lax.broadcasted_iota(jnp.int32, (1, LANES), 1)
    head_d = lane & (HEAD_DIM - 1)
    cos_t = table[:, 0:LANES] + jnp.where(head_d >= ROT_DIM, 1.0, 0.0)
    sin_t = table[:, LANES:2 * LANES]
    first_half = head_d < ROT_HALF
    lo_lanes = lane < HEAD_DIM

    def rotary(t):
        partner = jnp.where(first_half, pltpu.roll(t, LANES - ROT_HALF, 1),
                            pltpu.roll(t, ROT_HALF, 1))
        return t * cos_t + partner * sin_t

    k_rot = rotary(proj[:, ATTN_WIDTH:ATTN_WIDTH + KV_WIDTH])
    k_sw = pltpu.roll(k_rot, HEAD_DIM, 1)
    kd_ref[0, BLOCK:, :] = jnp.where(lo_lanes, k_rot, k_sw).astype(BF16)
    kd_ref[1, BLOCK:, :] = jnp.where(lo_lanes, k_sw, k_rot).astype(BF16)
    v = proj[:, ATTN_WIDTH + KV_WIDTH:ATTN_WIDTH + 2 * KV_WIDTH]
    vt_ref[:, BLOCK:] = v.T.astype(BF16)

    q_heads = []
    for c in range(ATTN_WIDTH // LANES):
        qc = rotary(proj[:, c * LANES:(c + 1) * LANES]) * (HEAD_DIM ** -0.5)
        q_heads.append(jnp.where(lo_lanes, qc, 0.0).astype(BF16))
        q_heads.append(jnp.where(lo_lanes, 0.0, qc).astype(BF16))

    bias_mid = bias_ref[0]
    bias_first = jnp.where(s_idx == 0, bias_ref[1], bias_mid)
    head_of_lane = lax.broadcasted_iota(jnp.int32, (1, GROUP * BLOCK), 1) // BLOCK
    for g in range(N_KV_HEADS):
        sink_row = jnp.zeros((1, GROUP * BLOCK), F32)
        for i in range(GROUP):
            sink_row = jnp.where(head_of_lane == i, sinks_ref[layer, g * GROUP + i], sink_row)
        for j in range(n_blk):
            rows = slice(j * BLOCK, (j + 1) * BLOCK)
            q4 = jnp.concatenate([q_heads[g * GROUP + i][rows] for i in range(GROUP)], axis=0)
            k_cat = kd_ref[g, j * BLOCK:(j + 2) * BLOCK, :]
            s_t = lax.dot_general(k_cat, q4, (((1,), (1,)), ((), ())),
                                  preferred_element_type=F32)
            bias = bias_first if j == 0 else bias_mid
            s_t = s_t + jnp.concatenate([bias] * GROUP, axis=1)
            m = jnp.maximum(jnp.max(s_t, axis=0, keepdims=True), sink_row)
            p_t = jnp.exp(s_t - m)
            denom = jnp.sum(p_t, axis=0, keepdims=True) + jnp.exp(sink_row - m)
            v_t = vt_ref[g * HEAD_DIM:(g + 1) * HEAD_DIM, j * BLOCK:(j + 2) * BLOCK]
            o_t = jnp.dot(v_t, p_t.astype(BF16), preferred_element_type=F32) * (1.0 / denom)
            for pair in range(GROUP // 2):
                both = jnp.concatenate(
                    [o_t[:, (2 * pair) * BLOCK:(2 * pair + 1) * BLOCK],
                     o_t[:, (2 * pair + 1) * BLOCK:(2 * pair + 2) * BLOCK]], axis=0)
                col = (g * GROUP // 2 + pair) * LANES
                cat_ref[rows, col:col + LANES] = both.T.astype(BF16)

    u = proj[:, ATTN_WIDTH + 2 * KV_WIDTH:]
    pool_ref[0, SUBLANES:, :] = u
    t_head = lax.broadcasted_iota(jnp.int32, (2 * SUBLANES, LANES), 0) + s_idx * tile
    for gi, w in enumerate(POOL_WINDOWS):
        cols = slice(gi * POOL_GROUP_WIDTH, (gi + 1) * POOL_GROUP_WIDTH)
        win_sum = u[:, cols]
        for n in range(gi + 1):
            back = 2 ** n
            if n > 0:
                pool_ref[n, SUBLANES:, cols] = win_sum
            win_sum = win_sum + pool_ref[n, SUBLANES - back:SUBLANES - back + tile, cols]
        ug = u[:, cols]
        count = jnp.minimum(t_head + 1, w).astype(F32)
        pooled = jnp.concatenate(
            [win_sum[0:2 * SUBLANES] / count, win_sum[2 * SUBLANES:] * (1.0 / w)], axis=0) - ug
        po = jnp.dot(pooled.astype(BF16), poolw_ref[gi], preferred_element_type=F32)
        cat_ref[:, ATTN_WIDTH + gi * POOL_GROUP_WIDTH:ATTN_WIDTH + (gi + 1) * POOL_GROUP_WIDTH] = (
            po * pscale_ref[:, cols]).astype(BF16)

    kd_ref[:, 0:BLOCK, :] = kd_ref[:, tile:tile + BLOCK, :]
    vt_ref[:, 0:BLOCK] = vt_ref[:, tile:tile + BLOCK]
    for n in range(n_stage):
        c0 = n * POOL_GROUP_WIDTH
        pool_ref[n, 0:SUBLANES, c0:] = pool_ref[n, tile:tile + SUBLANES, c0:]

    mix = jnp.dot(cat_ref[...], wout_ref[...], preferred_element_type=F32)
    o_ref[...] = x + gate * (_rms(mix) * gpost_ref[...])


def _mixer(layer, x, mod, pos3, w_in, b_in, sinks, pool_w, pool_scale, w_out,
           g_pre, g_post, rope_e, bias):
    batch, seq, _ = x.shape
    tile = MIXER_TILE
    const = lambda *shape: pl.BlockSpec(shape, lambda b, s: (0,) * len(shape))
    layer_spec = lambda *shape: pl.BlockSpec((None,) + shape, lambda b, s: (layer,) + (0,) * len(shape))
    return pl.pallas_call(
        functools.partial(_mixer_kernel, tile=tile, layer=layer),
        grid=(batch, seq // tile),
        in_specs=[
            pl.BlockSpec(memory_space=pltpu.SMEM),
            pl.BlockSpec((None, tile, D_MODEL), lambda b, s: (b, s, 0)),
            pl.BlockSpec((None, None, N_MOD, D_MODEL), lambda b, s: (layer, b, 0, 0)),
            pl.BlockSpec((None, 1, tile), lambda b, s: (b, 0, s)),
            layer_spec(D_MODEL, IN_WIDTH),
            layer_spec(1, IN_WIDTH),
            layer_spec(len(POOL_WINDOWS), POOL_GROUP_WIDTH, POOL_GROUP_WIDTH),
            layer_spec(1, POOL_WIDTH),
            layer_spec(D_MODEL, D_MODEL),
            layer_spec(1, D_MODEL),
            layer_spec(1, D_MODEL),
            const(4 * ROT_HALF, 2 * LANES),
            const(2, 2 * BLOCK, BLOCK),
        ],
        out_specs=pl.BlockSpec((None, tile, D_MODEL), lambda b, s: (b, s, 0)),
        out_shape=jax.ShapeDtypeStruct(x.shape, F32),
        scratch_shapes=[
            pltpu.VMEM((N_KV_HEADS, BLOCK + tile, LANES), BF16),
            pltpu.VMEM((KV_WIDTH, BLOCK + tile), BF16),
            pltpu.VMEM((len(POOL_WINDOWS), SUBLANES + tile, POOL_WIDTH), F32),
            pltpu.VMEM((tile, D_MODEL), BF16),
        ],
        compiler_params=pltpu.CompilerParams(
            dimension_semantics=("arbitrary", "arbitrary"),
            vmem_limit_bytes=VMEM_LIMIT_BYTES),
        name=f"mixer_l{layer}",
    )(sinks, x, mod, pos3, w_in, b_in, pool_w, pool_scale, w_out, g_pre, g_post, rope_e, bias)


def _ffn_kernel(x_ref, mod_ref, wg_ref, wu_ref, wd_ref, gpre_ref, gpost_ref, o_ref, acc_ref):
    x = x_ref[...]
    mod = mod_ref[...]
    shift, scale, gate = mod[3:4], mod[4:5], mod[5:6]
    h = (_rms(x) * (gpre_ref[...] * (1.0 + scale)) + shift).astype(BF16)
    for idx, c0 in enumerate(range(0, D_FF, FFN_CHUNK)):
        cols = slice(c0, c0 + FFN_CHUNK)
        gate_h = jnp.dot(h, wg_ref[:, cols], preferred_element_type=F32)
        up_h = jnp.dot(h, wu_ref[:, cols], preferred_element_type=F32)
        act = (gate_h * (1.0 / (1.0 + jnp.exp(-gate_h))) * up_h).astype(BF16)
        part = jnp.dot(act, wd_ref[cols, :], preferred_element_type=F32)
        if idx == 0:
            acc_ref[...] = part
        else:
            acc_ref[...] += part
    o_ref[...] = x + gate * (_rms(acc_ref[...]) * gpost_ref[...])


def _ffn(layer, x, mod, w_gate, w_up, w_down, g_pre, g_post):
    batch, seq, _ = x.shape
    tile = FFN_TILE
    resident = lambda *shape: pl.BlockSpec(
        (None,) + shape, lambda b, s: (layer,) + (0,) * len(shape), pipeline_mode=pl.Buffered(1))
    return pl.pallas_call(
        _ffn_kernel,
        grid=(batch, seq // tile),
        in_specs=[
            pl.BlockSpec((None, tile, D_MODEL), lambda b, s: (b, s, 0)),
            pl.BlockSpec((None, None, N_MOD, D_MODEL), lambda b, s: (layer, b, 0, 0)),
            resident(D_MODEL, D_FF),
            resident(D_MODEL, D_FF),
            resident(D_FF, D_MODEL),
            resident(1, D_MODEL),
            resident(1, D_MODEL),
        ],
        out_specs=pl.BlockSpec((None, tile, D_MODEL), lambda b, s: (b, s, 0)),
        out_shape=jax.ShapeDtypeStruct(x.shape, F32),
        scratch_shapes=[pltpu.VMEM((tile, D_MODEL), F32)],
        compiler_params=pltpu.CompilerParams(
            dimension_semantics=("arbitrary", "arbitrary"),
            vmem_limit_bytes=VMEM_LIMIT_BYTES),
        name=f"ffn_l{layer}",
    )(x, mod, w_gate, w_up, w_down, g_pre, g_post)


def kernel(x, c, positions, ada_w, ada_b, w_in, b_in, sinks, pool_w, pool_scale,
           w_out, w_gate, w_up, w_down, g_pre_mix, g_post_mix, g_pre_ffn, g_post_ffn):
    batch, seq, d_model = x.shape
    assert d_model == D_MODEL and seq % MIXER_TILE == 0 and seq % FFN_TILE == 0
    assert w_in.shape == (DEPTH, D_MODEL, IN_WIDTH) and w_gate.shape == (DEPTH, D_MODEL, D_FF)

    mod = _modulation(c, ada_w, ada_b).reshape(DEPTH, batch, N_MOD, D_MODEL)
    pos3 = positions.reshape(batch, 1, seq)
    rope_e = jnp.asarray(_rope_expand_matrix(), BF16)
    bias = jnp.asarray(_band_bias_t())
    row = lambda a: a.reshape(DEPTH, 1, a.shape[-1])

    w_in_b, w_out_b, pool_w_b = w_in.astype(BF16), w_out.astype(BF16), pool_w.astype(BF16)
    w_gate_b, w_up_b, w_down_b = w_gate.astype(BF16), w_up.astype(BF16), w_down.astype(BF16)
    for layer in range(DEPTH):
        x = _mixer(layer, x, mod, pos3, w_in_b, row(b_in), sinks, pool_w_b, row(pool_scale),
                   w_out_b, row(g_pre_mix), row(g_post_mix), rope_e, bias)
        x = _ffn(layer, x, mod, w_gate_b, w_up_b, w_down_b, row(g_pre_ffn), row(g_post_ffn))
    return x
```
